```python
import jax, jax.numpy as jnp
from jax import lax
import numpy as np

D_MODEL = 1024
BATCH = 4
SEQ = 8192
DEPTH = 2

CHUNK = 64
N_EVEN = (DEPTH + 1) // 2
N_ODD = DEPTH // 2
D_MIX = D_MODEL
D_A = D_MIX // 2
A_HEADS = 8
A_HEAD_DIM = D_A // A_HEADS
A_CONV = 4
LRU_C = 8.0
D_B = D_MIX // 2
B_GROUPS = 8
B_GROUP_DIM = D_B // B_GROUPS
SGU_BLOCK = 128
N_HEADS = 16
HEAD_DIM = D_MODEL // N_HEADS
N_PREV_CHUNKS = 8
BAND = (N_PREV_CHUNKS + 1) * CHUNK
MAX_REL = 256
N_REL = 2 * MAX_REL + 1
D_FF = 2752
FFN_CONV = 3
EPS = 1e-6
NEG_INF = -1e30

kernel_name = "hybrid_rglru_sgu_bandattn_convffn"


def rms_norm(x, g):
    xf = x.astype(jnp.float32)
    y = xf * lax.rsqrt(jnp.mean(xf * xf, axis=-1, keepdims=True) + EPS)
    return (y * g.astype(jnp.float32)).astype(x.dtype)


def layer_norm(x, g, b):
    xf = x.astype(jnp.float32)
    mu = jnp.mean(xf, axis=-1, keepdims=True)
    xc = xf - mu
    var = jnp.mean(xc * xc, axis=-1, keepdims=True)
    y = xc * lax.rsqrt(var + EPS) * g.astype(jnp.float32) + b.astype(jnp.float32)
    return y.astype(x.dtype)


def causal_dwconv(x, w, b):
    k, c = w.shape
    y = lax.conv_general_dilated(
        x, w[:, None, :].astype(x.dtype), window_strides=(1,), padding=[(k - 1, 0)],
        dimension_numbers=('NWC', 'WIO', 'NWC'), feature_group_count=c)
    return y + b


def rg_lru(x, w_r, b_r, w_i, b_i, lam):
    bsz, s, _ = x.shape
    xh = x.reshape(bsz, s, A_HEADS, A_HEAD_DIM)
    r = jax.nn.sigmoid((jnp.einsum('bshi,hij->bshj', xh, w_r).reshape(bsz, s, D_A) + b_r).astype(jnp.float32))
    i = jax.nn.sigmoid((jnp.einsum('bshi,hij->bshj', xh, w_i).reshape(bsz, s, D_A) + b_i).astype(jnp.float32))
    log_a = -LRU_C * r * jax.nn.softplus(-lam.astype(jnp.float32))
    a = jnp.exp(log_a)
    u = jnp.sqrt(-jnp.expm1(2.0 * log_a)) * (i * x.astype(jnp.float32))

    def combine(left, right):
        a1, b1 = left
        a2, b2 = right
        return a1 * a2, a2 * b1 + b2

    _, h = lax.associative_scan(combine, (a, u), axis=1)
    return h.astype(x.dtype)


def spatial_gating(u, v, ln_g, ln_b, w_s, b_s):
    bsz, s, _ = v.shape
    v = layer_norm(v, ln_g, ln_b)
    nb = s // SGU_BLOCK
    vb = v.reshape(bsz, nb, SGU_BLOCK, B_GROUPS, B_GROUP_DIM)
    chunk_id = jnp.arange(SGU_BLOCK) // CHUNK
    mask = (chunk_id[:, None] >= chunk_id[None, :]).astype(w_s.dtype)
    w = w_s * mask[None]
    mixed = jnp.einsum('gij,bnjgc->bnigc', w, vb) + jnp.transpose(b_s)[None, None, :, :, None]
    return u * mixed.reshape(bsz, s, D_B)


def rglru_sgu_mixer(h, w_in, conv_w, conv_b, w_r, b_r, w_i, b_i, lam, ln_g, ln_b, w_s, b_s, w_out):
    z = h @ w_in
    a_x, a_g, b_u, b_v = jnp.split(z, 4, axis=-1)
    a_x = causal_dwconv(a_x, conv_w, conv_b)
    y_a = rg_lru(a_x, w_r, b_r, w_i, b_i, lam) * jax.nn.gelu(a_g)
    y_b = spatial_gating(jax.nn.gelu(b_u), jax.nn.gelu(b_v), ln_g, ln_b, w_s, b_s)
    return jnp.concatenate([y_a, y_b], axis=-1) @ w_out


def chunk_band_attention(h, w_qkv, rel_bias, w_o):
    bsz, s, _ = h.shape
    nc = s // CHUNK
    qkv = (h @ w_qkv).reshape(bsz, s, 3, N_HEADS, HEAD_DIM)
    q, k, v = qkv[:, :, 0], qkv[:, :, 1], qkv[:, :, 2]
    pad = N_PREV_CHUNKS * CHUNK
    kp = jnp.pad(k, ((0, 0), (pad, 0), (0, 0), (0, 0)))
    vp = jnp.pad(v, ((0, 0), (pad, 0), (0, 0), (0, 0)))
    qi = jnp.arange(CHUNK)
    ki = jnp.arange(BAND)
    rel = jnp.clip(pad + qi[:, None] - ki[None, :], -MAX_REL, MAX_REL) + MAX_REL
    bias = rel_bias[:, rel].astype(jnp.float32)
    scale = HEAD_DIM ** -0.5

    def one_chunk(c):
        start = c * CHUNK
        qc = lax.dynamic_slice_in_dim(q, start, CHUNK, axis=1)
        kc = lax.dynamic_slice_in_dim(kp, start, BAND, axis=1)
        vc = lax.dynamic_slice_in_dim(vp, start, BAND, axis=1)
        sc = jnp.einsum('bqhd,bkhd->bhqk', qc, kc).astype(jnp.float32) * scale + bias[None]
        valid = ki >= pad - start
        sc = jnp.where(valid[None, None, None, :], sc, NEG_INF)
        p = jax.nn.softmax(sc, axis=-1).astype(vc.dtype)
        return jnp.einsum('bhqk,bkhd->bqhd', p, vc)

    out = lax.map(one_chunk, jnp.arange(nc))
    out = jnp.moveaxis(out, 0, 1).reshape(bsz, s, D_MODEL)
    return out @ w_o


def conv_ffn(h, w_up, conv_w, conv_b, w_down):
    z = causal_dwconv(h @ w_up, conv_w, conv_b)
    g, u = jnp.split(z, 2, axis=-1)
    return (jax.nn.silu(g) * u) @ w_down


def setup_inputs(seed: int = 0) -> dict:
    key = jax.random.key(seed)
    ks = jax.random.split(key, 32)
    f32 = jnp.float32

    def nrm(k, shape, scale):
        return jax.random.normal(k, shape, f32) * scale

    def gain(k, shape):
        return 1.0 + 0.02 * jax.random.normal(k, shape, f32)

    x = jax.random.normal(ks[0], (BATCH, SEQ, D_MODEL), f32)
    norm_ab = gain(ks[1], (N_EVEN, D_MODEL))
    w_in_ab = nrm(ks[2], (N_EVEN, D_MODEL, 4 * D_A), D_MODEL ** -0.5)
    lru_conv_w = nrm(ks[3], (N_EVEN, A_CONV, D_A), A_CONV ** -0.5)
    lru_conv_b = nrm(ks[4], (N_EVEN, D_A), 0.02)
    lru_gate_r_w = nrm(ks[5], (N_EVEN, A_HEADS, A_HEAD_DIM, A_HEAD_DIM), A_HEAD_DIM ** -0.5)
    lru_gate_r_b = nrm(ks[6], (N_EVEN, D_A), 0.02)
    lru_gate_i_w = nrm(ks[7], (N_EVEN, A_HEADS, A_HEAD_DIM, A_HEAD_DIM), A_HEAD_DIM ** -0.5)
    lru_gate_i_b = nrm(ks[8], (N_EVEN, D_A), 0.02)
    a_pow = jax.random.uniform(ks[9], (N_EVEN, D_A), f32, minval=0.9, maxval=0.999)
    a_base = a_pow ** (1.0 / LRU_C)
    lru_lambda = jnp.log(a_base) - jnp.log1p(-a_base)
    sgu_ln_g = gain(ks[10], (N_EVEN, D_B))
    sgu_ln_b = nrm(ks[11], (N_EVEN, D_B), 0.02)
    sgu_w = nrm(ks[12], (N_EVEN, B_GROUPS, SGU_BLOCK, SGU_BLOCK), SGU_BLOCK ** -0.5)
    sgu_b = 1.0 + 0.02 * jax.random.normal(ks[13], (N_EVEN, B_GROUPS, SGU_BLOCK), f32)
    w_out_ab = nrm(ks[14], (N_EVEN, D_MIX, D_MODEL), D_MIX ** -0.5)
    norm_attn = gain(ks[15], (N_ODD, D_MODEL))
    w_qkv = nrm(ks[16], (N_ODD, D_MODEL, 3 * D_MODEL), D_MODEL ** -0.5)
    rel_bias = nrm(ks[17], (N_ODD, N_HEADS, N_REL), 0.1)
    w_o = nrm(ks[18], (N_ODD, D_MODEL, D_MODEL), D_MODEL ** -0.5)
    norm_ffn = gain(ks[19], (DEPTH, D_MODEL))
    ffn_w_up = nrm(ks[20], (DEPTH, D_MODEL, 2 * D_FF), D_MODEL ** -0.5)
    ffn_conv_w = nrm(ks[21], (DEPTH, FFN_CONV, 2 * D_FF), FFN_CONV ** -0.5)
    ffn_conv_b = nrm(ks[22], (DEPTH, 2 * D_FF), 0.02)
    ffn_w_down = nrm(ks[23], (DEPTH, D_FF, D_MODEL), D_FF ** -0.5)
    norm_final = gain(ks[24], (D_MODEL,))
    return {
        "x": x,
        "norm_ab": norm_ab, "w_in_ab": w_in_ab,
        "lru_conv_w": lru_conv_w, "lru_conv_b": lru_conv_b,
        "lru_gate_r_w": lru_gate_r_w, "lru_gate_r_b": lru_gate_r_b,
        "lru_gate_i_w": lru_gate_i_w, "lru_gate_i_b": lru_gate_i_b,
        "lru_lambda": lru_lambda,
        "sgu_ln_g": sgu_ln_g, "sgu_ln_b": sgu_ln_b, "sgu_w": sgu_w, "sgu_b": sgu_b,
        "w_out_ab": w_out_ab,
        "norm_attn": norm_attn, "w_qkv": w_qkv, "rel_bias": rel_bias, "w_o": w_o,
        "norm_ffn": norm_ffn, "ffn_w_up": ffn_w_up, "ffn_conv_w": ffn_conv_w,
        "ffn_conv_b": ffn_conv_b, "ffn_w_down": ffn_w_down,
        "norm_final": norm_final,
    }


def reference(x, norm_ab, w_in_ab, lru_conv_w, lru_conv_b, lru_gate_r_w, lru_gate_r_b,
              lru_gate_i_w, lru_gate_i_b, lru_lambda, sgu_ln_g, sgu_ln_b, sgu_w, sgu_b,
              w_out_ab, norm_attn, w_qkv, rel_bias, w_o, norm_ffn, ffn_w_up, ffn_conv_w,
              ffn_conv_b, ffn_w_down, norm_final):
    for layer in range(DEPTH):
        if layer % 2 == 0:
            e = layer // 2
            x = x + rglru_sgu_mixer(
                rms_norm(x, norm_ab[e]), w_in_ab[e], lru_conv_w[e], lru_conv_b[e],
                lru_gate_r_w[e], lru_gate_r_b[e], lru_gate_i_w[e], lru_gate_i_b[e],
                lru_lambda[e], sgu_ln_g[e], sgu_ln_b[e], sgu_w[e], sgu_b[e], w_out_ab[e])
        else:
            o = layer // 2
            x = x + chunk_band_attention(rms_norm(x, norm_attn[o]), w_qkv[o], rel_bias[o], w_o[o])
        x = x + conv_ffn(rms_norm(x, norm_ffn[layer]), ffn_w_up[layer], ffn_conv_w[layer],
                         ffn_conv_b[layer], ffn_w_down[layer])
    return rms_norm(x, norm_final)
```

```python
import functools

import jax
import jax.numpy as jnp
from jax import lax
from jax.experimental import pallas as pl
from jax.experimental.pallas import tpu as pltpu

EPS = 1e-6
NEG_INF = -1e30
LRU_C = 8.0

CHUNK = 64
N_PREV_CHUNKS = 8
MAX_REL = 256
SGU_BLOCK = 128
N_HEADS = 16
A_HEADS = 8
B_GROUPS = 8

SUBLANES = 8
LANES = 128
MXU_DIM = 256

FFN_TILE = 512
FFN_CHUNK = MXU_DIM
MIX_TILE = 256
ATTN_TILE = 2 * CHUNK
VMEM_LIMIT = 56 * 1024 * 1024

F32 = jnp.float32
BF16 = jnp.bfloat16


def _rms(x, g):
    ms = jnp.mean(x * x, axis=-1, keepdims=True)
    return x * lax.rsqrt(ms + EPS) * g


def _shift_rows(z, prev, d):
    zr = pltpu.roll(z, d, 0)
    pr = pltpu.roll(prev, d, 0)
    row = lax.broadcasted_iota(jnp.int32, prev.shape, 0)
    head = jnp.where(row < d, pr, zr[:SUBLANES])
    return jnp.concatenate([head, zr[SUBLANES:]], axis=0)


def _causal_conv(z, prev, w_ref, b_ref, cols):
    taps = w_ref.shape[0]
    out = w_ref[taps - 1:taps, cols] * z + b_ref[:, cols]
    for d in range(1, taps):
        out = out + w_ref[taps - 1 - d:taps - d, cols] * _shift_rows(z, prev, d)
    return out


def _resident(shape):
    nd = len(shape)
    return pl.BlockSpec(shape, lambda b, s: (0,) * nd, pipeline_mode=pl.Buffered(1))


def _ffn_kernel(x_ref, g_ref, wg_ref, wu_ref, cwg_ref, cwu_ref, cbg_ref, cbu_ref, wd_ref, gf_ref,
                o_ref, pg_scr, pu_scr, acc_scr, *, final_norm):
    @pl.when(pl.program_id(1) == 0)
    def _():
        pg_scr[...] = jnp.zeros_like(pg_scr)
        pu_scr[...] = jnp.zeros_like(pu_scr)

    x = x_ref[0]
    t = x.shape[0]
    h = _rms(x, g_ref[...]).astype(BF16)
    n_chunks = wg_ref.shape[1] // FFN_CHUNK
    for f in range(n_chunks):
        cols = slice(f * FFN_CHUNK, (f + 1) * FFN_CHUNK)
        zg = jnp.dot(h, wg_ref[:, cols], preferred_element_type=F32)
        zu = jnp.dot(h, wu_ref[:, cols], preferred_element_type=F32)
        pg = pg_scr[:, cols]
        pu = pu_scr[:, cols]
        pg_scr[:, cols] = zg[t - SUBLANES:]
        pu_scr[:, cols] = zu[t - SUBLANES:]
        a = _causal_conv(zg, pg, cwg_ref, cbg_ref, cols)
        u = _causal_conv(zu, pu, cwu_ref, cbu_ref, cols)
        act = (a * jax.nn.sigmoid(a) * u).astype(BF16)
        part = jnp.dot(act, wd_ref[cols, :], preferred_element_type=F32)
        if f == 0:
            acc_scr[...] = part
        else:
            acc_scr[...] += part
    y = x + acc_scr[...]
    if final_norm:
        y = _rms(y, gf_ref[...])
    o_ref[0] = y


def _ffn(x, g, wg, wu, cwg, cwu, cbg, cbu, wd, gf, *, final_norm):
    bsz, seq, d = x.shape
    fp = wg.shape[1]
    tile = min(FFN_TILE, seq)
    assert seq % tile == 0 and fp % FFN_CHUNK == 0
    return pl.pallas_call(
        functools.partial(_ffn_kernel, final_norm=final_norm),
        out_shape=jax.ShapeDtypeStruct(x.shape, x.dtype),
        grid=(bsz, seq // tile),
        in_specs=[
            pl.BlockSpec((1, tile, d), lambda b, s: (b, s, 0)),
            _resident(g.shape), _resident(wg.shape), _resident(wu.shape),
            _resident(cwg.shape), _resident(cwu.shape), _resident(cbg.shape), _resident(cbu.shape),
            _resident(wd.shape), _resident(gf.shape),
        ],
        out_specs=pl.BlockSpec((1, tile, d), lambda b, s: (b, s, 0)),
        scratch_shapes=[
            pltpu.VMEM((SUBLANES, fp), F32),
            pltpu.VMEM((SUBLANES, fp), F32),
            pltpu.VMEM((tile, d), F32),
        ],
        compiler_params=pltpu.CompilerParams(
            dimension_semantics=("arbitrary", "arbitrary"), vmem_limit_bytes=VMEM_LIMIT),
        name="conv_ffn",
    )(x, g, wg, wu, cwg, cwu, cbg, cbu, wd, gf)


def _scan_within_vregs(a3, b3):
    sub = lax.broadcasted_iota(jnp.int32, a3.shape, 1)
    d = 1
    while d < SUBLANES:
        keep = sub >= d
        a_prev = jnp.where(keep, pltpu.roll(a3, d, 1), 1.0)
        b_prev = jnp.where(keep, pltpu.roll(b3, d, 1), 0.0)
        b3 = a3 * b_prev + b3
        a3 = a3 * a_prev
        d *= 2
    return a3, b3


def _mixer_kernel(x_ref, g_ref, win_ref, cw_ref, cb_ref, wri_ref, bri_ref, lam_ref, lng_ref, lnb_ref,
                  ws_ref, bs_ref, wout_ref, o_ref, px_scr, st_scr, y_scr):
    da = cw_ref.shape[1]

    @pl.when(pl.program_id(1) == 0)
    def _():
        px_scr[...] = jnp.zeros_like(px_scr)
        st_scr[...] = jnp.zeros_like(st_scr)

    x = x_ref[0]
    t = x.shape[0]
    h = _rms(x, g_ref[...]).astype(BF16)
    z = jnp.dot(h, win_ref[...], preferred_element_type=F32)
    a_x, a_g, b_u, b_v = (z[:, i * da:(i + 1) * da] for i in range(4))

    prev = px_scr[...]
    px_scr[...] = a_x[t - SUBLANES:]
    xc = _causal_conv(a_x, prev, cw_ref, cb_ref, slice(0, da))
    gates = jnp.dot(xc.astype(BF16), wri_ref[...], preferred_element_type=F32) + bri_ref[...]
    r = jax.nn.sigmoid(gates[:, :da])
    i = jax.nn.sigmoid(gates[:, da:])
    log_a = -LRU_C * r * jax.nn.softplus(-lam_ref[...])
    a = jnp.exp(log_a)
    u = jnp.sqrt(-jnp.tanh(log_a) * (1.0 + a * a)) * (i * xc)
    n = t // SUBLANES
    a3, b3 = _scan_within_vregs(a.reshape(n, SUBLANES, da), u.reshape(n, SUBLANES, da))
    carry = st_scr[...]
    rows = []
    for k in range(n):
        hk = b3[k] + a3[k] * carry
        rows.append(hk)
        carry = jnp.broadcast_to(hk[SUBLANES - 1:], hk.shape)
    st_scr[...] = carry
    y_a = jnp.concatenate(rows, axis=0) * jax.nn.gelu(a_g)
    y_scr[:, :da] = y_a.astype(BF16)

    ug = jax.nn.gelu(b_u)
    v = jax.nn.gelu(b_v)
    mu = jnp.mean(v, axis=-1, keepdims=True)
    vc = v - mu
    var = jnp.mean(vc * vc, axis=-1, keepdims=True)
    vn = (vc * lax.rsqrt(var + EPS) * lng_ref[...] + lnb_ref[...]).astype(BF16)
    ri = lax.broadcasted_iota(jnp.int32, (SGU_BLOCK, SGU_BLOCK), 0) // CHUNK
    ci = lax.broadcasted_iota(jnp.int32, (SGU_BLOCK, SGU_BLOCK), 1) // CHUNK
    chunk_causal = ri >= ci
    n_groups = ws_ref.shape[0]
    gdim = da // n_groups
    per_tile = LANES // gdim
    wm = [jnp.where(chunk_causal, ws_ref[g], 0.0).astype(BF16) for g in range(n_groups)]
    lane = lax.broadcasted_iota(jnp.int32, (SGU_BLOCK, LANES), 1)
    for blk in range(t // SGU_BLOCK):
        rws = slice(blk * SGU_BLOCK, (blk + 1) * SGU_BLOCK)
        tiles = []
        for lt in range(da // LANES):
            vt = vn[rws, lt * LANES:(lt + 1) * LANES]
            mixed = jnp.dot(wm[lt * per_tile], vt, preferred_element_type=F32)
            for j in range(1, per_tile):
                mj = jnp.dot(wm[lt * per_tile + j], vt, preferred_element_type=F32)
                mixed = jnp.where(lane >= j * gdim, mj, mixed)
            tiles.append(mixed)
        mixed = jnp.concatenate(tiles, axis=1) + bs_ref[...]
        y_scr[rws, da:] = (ug[rws] * mixed).astype(BF16)

    o_ref[0] = x + jnp.dot(y_scr[...], wout_ref[...], preferred_element_type=F32)


def _mixer(x, g, win, cw, cb, wri, bri, lam, lng, lnb, ws, bs, wout):
    bsz, seq, d = x.shape
    da = cw.shape[1]
    tile = min(MIX_TILE, seq)
    assert seq % tile == 0 and tile % SGU_BLOCK == 0
    consts = (g, win, cw, cb, wri, bri, lam, lng, lnb, ws, bs, wout)
    return pl.pallas_call(
        _mixer_kernel,
        out_shape=jax.ShapeDtypeStruct(x.shape, x.dtype),
        grid=(bsz, seq // tile),
        in_specs=[pl.BlockSpec((1, tile, d), lambda b, s: (b, s, 0))] + [_resident(c.shape) for c in consts],
        out_specs=pl.BlockSpec((1, tile, d), lambda b, s: (b, s, 0)),
        scratch_shapes=[
            pltpu.VMEM((SUBLANES, da), F32),
            pltpu.VMEM((SUBLANES, da), F32),
            pltpu.VMEM((tile, 2 * da), BF16),
        ],
        compiler_params=pltpu.CompilerParams(
            dimension_semantics=("arbitrary", "arbitrary"), vmem_limit_bytes=VMEM_LIMIT),
        name="rglru_sgu_mixer",
    )(x, *consts)


def _attn_kernel(x_ref, g_ref, wqkv_ref, bias_ref, wo_ref, o_ref, k_scr, v_scr, a_scr):
    s = pl.program_id(1)
    x = x_ref[0]
    tq, d = x.shape
    n_heads = bias_ref.shape[0]
    hd = d // n_heads
    band = k_scr.shape[0]
    past = band - tq

    h = _rms(x, g_ref[...]).astype(BF16)
    q = (jnp.dot(h, wqkv_ref[:, :d], preferred_element_type=F32) * hd ** -0.5).astype(BF16)
    k = jnp.dot(h, wqkv_ref[:, d:2 * d], preferred_element_type=F32).astype(BF16)
    v = jnp.dot(h, wqkv_ref[:, 2 * d:], preferred_element_type=F32).astype(BF16)

    @pl.when(s == 0)
    def _():
        k_scr[:past] = jnp.zeros((past, d), BF16)
        v_scr[:past] = jnp.zeros((past, d), BF16)

    @pl.when(s > 0)
    def _():
        k_scr[:past] = k_scr[tq:]
        v_scr[:past] = v_scr[tq:]

    k_scr[past:] = k
    v_scr[past:] = v

    kpos = lax.broadcasted_iota(jnp.int32, (tq, band), 1)
    valid = kpos >= past - s * tq
    lane = lax.broadcasted_iota(jnp.int32, (tq, MXU_DIM), 1)
    per_tile = MXU_DIM // hd
    for ht in range(d // MXU_DIM):
        cols = slice(ht * MXU_DIM, (ht + 1) * MXU_DIM)
        qt = q[:, cols]
        kt = k_scr[:, cols]
        vt = v_scr[:, cols]
        out = jnp.zeros((tq, MXU_DIM), F32)
        for j in range(per_tile):
            mine = (lane >= j * hd) & (lane < (j + 1) * hd)
            qj = jnp.where(mine, qt, jnp.zeros_like(qt))
            sc = lax.dot_general(qj, kt, (((1,), (1,)), ((), ())), preferred_element_type=F32)
            sc = jnp.where(valid, sc + bias_ref[ht * per_tile + j], NEG_INF)
            m = jnp.max(sc, axis=-1, keepdims=True)
            p = jnp.exp(sc - m)
            l = jnp.sum(p, axis=-1, keepdims=True)
            o = jnp.dot(p.astype(BF16), vt, preferred_element_type=F32) * (1.0 / l)
            out = jnp.where(mine, o, out)
        a_scr[:, cols] = out.astype(BF16)

    o_ref[0] = x + jnp.dot(a_scr[...], wo_ref[...], preferred_element_type=F32)


def _attn(x, g, wqkv, bias, wo):
    bsz, seq, d = x.shape
    tq = bias.shape[1]
    band = bias.shape[2]
    assert seq % tq == 0
    consts = (g, wqkv, bias, wo)
    return pl.pallas_call(
        _attn_kernel,
        out_shape=jax.ShapeDtypeStruct(x.shape, x.dtype),
        grid=(bsz, seq // tq),
        in_specs=[pl.BlockSpec((1, tq, d), lambda b, s: (b, s, 0))] + [_resident(c.shape) for c in consts],
        out_specs=pl.BlockSpec((1, tq, d), lambda b, s: (b, s, 0)),
        scratch_shapes=[
            pltpu.VMEM((band, d), BF16),
            pltpu.VMEM((band, d), BF16),
            pltpu.VMEM((tq, d), BF16),
        ],
        compiler_params=pltpu.CompilerParams(
            dimension_semantics=("arbitrary", "arbitrary"), vmem_limit_bytes=VMEM_LIMIT),
        name="band_attention",
    )(x, *consts)


def _band_bias(rel_bias, tq):
    past = N_PREV_CHUNKS * CHUNK
    qi = jnp.arange(tq)
    ki = jnp.arange(past + tq)
    rel = jnp.clip(past + qi[:, None] - ki[None, :], -MAX_REL, MAX_REL) + MAX_REL
    dc = ki[None, :] // CHUNK - qi[:, None] // CHUNK
    in_band = (dc >= 0) & (dc <= N_PREV_CHUNKS)
    return jnp.where(in_band[None], rel_bias[:, rel].astype(F32), NEG_INF)


def _row(v):
    return v.reshape(1, -1)


def _pad_cols(w, n):
    return jnp.pad(w, ((0, 0), (0, n - w.shape[1])))


def _ffn_params(norm, w_up, conv_w, conv_b, w_down):
    dff = w_down.shape[0]
    fp = -(-dff // FFN_CHUNK) * FFN_CHUNK
    wg = _pad_cols(w_up[:, :dff].astype(BF16), fp)
    wu = _pad_cols(w_up[:, dff:].astype(BF16), fp)
    cwg = _pad_cols(conv_w[:, :dff], fp)
    cwu = _pad_cols(conv_w[:, dff:], fp)
    cbg = _pad_cols(_row(conv_b[:dff]), fp)
    cbu = _pad_cols(_row(conv_b[dff:]), fp)
    wd = jnp.pad(w_down.astype(BF16), ((0, fp - dff), (0, 0)))
    return _row(norm), wg, wu, cwg, cwu, cbg, cbu, wd


def _block_diag(w):
    nh, di, dj = w.shape
    eye = jnp.eye(nh, dtype=w.dtype)
    return (eye[:, None, :, None] * w[:, :, None, :]).reshape(nh * di, nh * dj)


def kernel(x, norm_ab, w_in_ab, lru_conv_w, lru_conv_b, lru_gate_r_w, lru_gate_r_b, lru_gate_i_w, lru_gate_i_b, lru_lambda, sgu_ln_g, sgu_ln_b, sgu_w, sgu_b, w_out_ab, norm_attn, w_qkv, rel_bias, w_o, norm_ffn, ffn_w_up, ffn_conv_w, ffn_conv_b, ffn_w_down, norm_final):
    depth = norm_ffn.shape[0]
    for layer in range(depth):
        if layer % 2 == 0:
            e = layer // 2
            wri = jnp.concatenate([_block_diag(lru_gate_r_w[e]), _block_diag(lru_gate_i_w[e])], axis=1)
            bri = _row(jnp.concatenate([lru_gate_r_b[e], lru_gate_i_b[e]]))
            gdim = sgu_ln_g.shape[1] // sgu_w.shape[1]
            bs = jnp.repeat(sgu_b[e].T, gdim, axis=1)
            x = _mixer(x, _row(norm_ab[e]), w_in_ab[e].astype(BF16), lru_conv_w[e], _row(lru_conv_b[e]),
                       wri.astype(BF16), bri, _row(lru_lambda[e]), _row(sgu_ln_g[e]), _row(sgu_ln_b[e]),
                       sgu_w[e], bs, w_out_ab[e].astype(BF16))
        else:
            o = layer // 2
            x = _attn(x, _row(norm_attn[o]), w_qkv[o].astype(BF16), _band_bias(rel_bias[o], ATTN_TILE),
                      w_o[o].astype(BF16))
        last = layer == depth - 1
        x = _ffn(x, *_ffn_params(norm_ffn[layer], ffn_w_up[layer], ffn_conv_w[layer], ffn_conv_b[layer],
                                 ffn_w_down[layer]), _row(norm_final), final_norm=last)
    return x
```

```python
import functools

import jax
import jax.numpy as jnp
from jax import lax
from jax.experimental import pallas as pl
from jax.experimental.pallas import tpu as pltpu

EPS = 1e-6
NEG_INF = -1e30
LRU_C = 8.0

CHUNK = 64
N_PREV_CHUNKS = 8
MAX_REL = 256
SGU_BLOCK = 128
N_HEADS = 16
A_HEADS = 8
B_GROUPS = 8

SUBLANES = 8
LANES = 128
MXU_DIM = 256

FFN_TILE = 512
FFN_CHUNK = MXU_DIM
MIX_TILE = 256
ATTN_TILE = 2 * CHUNK
VMEM_LIMIT = 56 * 1024 * 1024

F32 = jnp.float32
BF16 = jnp.bfloat16


def _rms(x, g):
    ms = jnp.mean(x * x, axis=-1, keepdims=True)
    return x * lax.rsqrt(ms + EPS) * g


def _shift_rows(z, prev, d):
    zr = pltpu.roll(z, d, 0)
    pr = pltpu.roll(prev, d, 0)
    row = lax.broadcasted_iota(jnp.int32, prev.shape, 0)
    head = jnp.where(row < d, pr, zr[:SUBLANES])
    return jnp.concatenate([head, zr[SUBLANES:]], axis=0)


def _causal_conv(z, prev, w_ref, b_ref, cols):
    taps = w_ref.shape[0]
    out = w_ref[taps - 1:taps, cols] * z + b_ref[:, cols]
    for d in range(1, taps):
        out = out + w_ref[taps - 1 - d:taps - d, cols] * _shift_rows(z, prev, d)
    return out


def _resident(shape):
    nd = len(shape)
    return pl.BlockSpec(shape, lambda b, s: (0,) * nd, pipeline_mode=pl.Buffered(1))


def _conv_rows(z_scr, t, w_ref, b_ref, cols):
    taps = w_ref.shape[0]
    out = w_ref[taps - 1:taps, cols] * z_scr[pl.ds(SUBLANES, t), :] + b_ref[:, cols]
    for d in range(1, taps):
        out = out + w_ref[taps - 1 - d:taps - d, cols] * z_scr[pl.ds(SUBLANES - d, t), :]
    return out


def _ffn_kernel(x_ref, g_ref, wg_ref, wu_ref, cwg_ref, cwu_ref, cbg_ref, cbu_ref, wd_ref, gf_ref,
                o_ref, pg_scr, pu_scr, h_scr, zg0, zg1, zu0, zu1, act0, act1, acc_scr, *, final_norm):
    @pl.when(pl.program_id(1) == 0)
    def _():
        pg_scr[...] = jnp.zeros_like(pg_scr)
        pu_scr[...] = jnp.zeros_like(pu_scr)

    zg_bufs, zu_bufs, act_bufs = (zg0, zg1), (zu0, zu1), (act0, act1)
    x = x_ref[0]
    t = x.shape[0]
    h_scr[...] = _rms(x, g_ref[...]).astype(BF16)
    n_chunks = wg_ref.shape[1] // FFN_CHUNK

    def up_g(f):
        cols = slice(f * FFN_CHUNK, (f + 1) * FFN_CHUNK)
        zg_bufs[f % 2][pl.ds(SUBLANES, t), :] = jnp.dot(h_scr[...], wg_ref[:, cols], preferred_element_type=F32)

    def up_u(f):
        cols = slice(f * FFN_CHUNK, (f + 1) * FFN_CHUNK)
        zu_bufs[f % 2][pl.ds(SUBLANES, t), :] = jnp.dot(h_scr[...], wu_ref[:, cols], preferred_element_type=F32)

    up_g(0)
    up_u(0)
    for f in range(n_chunks):
        cols = slice(f * FFN_CHUNK, (f + 1) * FFN_CHUNK)
        zg_scr, zu_scr, act_scr = zg_bufs[f % 2], zu_bufs[f % 2], act_bufs[f % 2]
        if f + 1 < n_chunks:
            up_g(f + 1)
        zg_scr[pl.ds(0, SUBLANES), :] = pg_scr[:, cols]
        pg_scr[:, cols] = zg_scr[pl.ds(t, SUBLANES), :]
        a = _conv_rows(zg_scr, t, cwg_ref, cbg_ref, cols)
        a = a * jax.nn.sigmoid(a)
        if f + 1 < n_chunks:
            up_u(f + 1)
        zu_scr[pl.ds(0, SUBLANES), :] = pu_scr[:, cols]
        pu_scr[:, cols] = zu_scr[pl.ds(t, SUBLANES), :]
        u = _conv_rows(zu_scr, t, cwu_ref, cbu_ref, cols)
        act_scr[...] = (a * u).astype(BF16)
        part = jnp.dot(act_scr[...], wd_ref[cols, :], preferred_element_type=F32)
        if f == 0:
            acc_scr[...] = part
        else:
            acc_scr[...] += part
    y = x + acc_scr[...]
    if final_norm:
        y = _rms(y, gf_ref[...])
    o_ref[0] = y


def _ffn(x, g, wg, wu, cwg, cwu, cbg, cbu, wd, gf, *, final_norm):
    bsz, seq, d = x.shape
    fp = wg.shape[1]
    tile = min(FFN_TILE, seq)
    assert seq % tile == 0 and fp % FFN_CHUNK == 0
    return pl.pallas_call(
        functools.partial(_ffn_kernel, final_norm=final_norm),
        out_shape=jax.ShapeDtypeStruct(x.shape, x.dtype),
        grid=(bsz, seq // tile),
        in_specs=[
            pl.BlockSpec((1, tile, d), lambda b, s: (b, s, 0)),
            _resident(g.shape), _resident(wg.shape), _resident(wu.shape),
            _resident(cwg.shape), _resident(cwu.shape), _resident(cbg.shape), _resident(cbu.shape),
            _resident(wd.shape), _resident(gf.shape),
        ],
        out_specs=pl.BlockSpec((1, tile, d), lambda b, s: (b, s, 0)),
        scratch_shapes=[
            pltpu.VMEM((SUBLANES, fp), F32),
            pltpu.VMEM((SUBLANES, fp), F32),
            pltpu.VMEM((tile, d), BF16),
            pltpu.VMEM((tile + SUBLANES, FFN_CHUNK), F32),
            pltpu.VMEM((tile + SUBLANES, FFN_CHUNK), F32),
            pltpu.VMEM((tile + SUBLANES, FFN_CHUNK), F32),
            pltpu.VMEM((tile + SUBLANES, FFN_CHUNK), F32),
            pltpu.VMEM((tile, FFN_CHUNK), BF16),
            pltpu.VMEM((tile, FFN_CHUNK), BF16),
            pltpu.VMEM((tile, d), F32),
        ],
        compiler_params=pltpu.CompilerParams(
            dimension_semantics=("arbitrary", "arbitrary"), vmem_limit_bytes=VMEM_LIMIT),
        name="conv_ffn",
    )(x, g, wg, wu, cwg, cwu, cbg, cbu, wd, gf)


def _scan_within_vregs(a3, b3):
    sub = lax.broadcasted_iota(jnp.int32, a3.shape, 1)
    d = 1
    while d < SUBLANES:
        keep = sub >= d
        a_prev = jnp.where(keep, pltpu.roll(a3, d, 1), 1.0)
        b_prev = jnp.where(keep, pltpu.roll(b3, d, 1), 0.0)
        b3 = a3 * b_prev + b3
        a3 = a3 * a_prev
        d *= 2
    return a3, b3


def _mixer_kernel(x_ref, g_ref, win_ref, cw_ref, cb_ref, wri_ref, bri_ref, lam_ref, lng_ref, lnb_ref,
                  ws_ref, bs_ref, wout_ref, o_ref, px_scr, st_scr, y_scr):
    da = cw_ref.shape[1]

    @pl.when(pl.program_id(1) == 0)
    def _():
        px_scr[...] = jnp.zeros_like(px_scr)
        st_scr[...] = jnp.zeros_like(st_scr)

    x = x_ref[0]
    t = x.shape[0]
    h = _rms(x, g_ref[...]).astype(BF16)
    z = jnp.dot(h, win_ref[...], preferred_element_type=F32)
    a_x, a_g, b_u, b_v = (z[:, i * da:(i + 1) * da] for i in range(4))

    prev = px_scr[...]
    px_scr[...] = a_x[t - SUBLANES:]
    xc = _causal_conv(a_x, prev, cw_ref, cb_ref, slice(0, da))
    gates = jnp.dot(xc.astype(BF16), wri_ref[...], preferred_element_type=F32) + bri_ref[...]
    r = jax.nn.sigmoid(gates[:, :da])
    i = jax.nn.sigmoid(gates[:, da:])
    log_a = -LRU_C * r * jax.nn.softplus(-lam_ref[...])
    a = jnp.exp(log_a)
    u = jnp.sqrt(-jnp.tanh(log_a) * (1.0 + a * a)) * (i * xc)
    n = t // SUBLANES
    a3, b3 = _scan_within_vregs(a.reshape(n, SUBLANES, da), u.reshape(n, SUBLANES, da))
    carry = st_scr[...]
    rows = []
    for k in range(n):
        hk = b3[k] + a3[k] * carry
        rows.append(hk)
        carry = jnp.broadcast_to(hk[SUBLANES - 1:], hk.shape)
    st_scr[...] = carry
    y_a = jnp.concatenate(rows, axis=0) * jax.nn.gelu(a_g)
    y_scr[:, :da] = y_a.astype(BF16)

    ug = jax.nn.gelu(b_u)
    v = jax.nn.gelu(b_v)
    mu = jnp.mean(v, axis=-1, keepdims=True)
    vc = v - mu
    var = jnp.mean(vc * vc, axis=-1, keepdims=True)
    vn = (vc * lax.rsqrt(var + EPS) * lng_ref[...] + lnb_ref[...]).astype(BF16)
    ri = lax.broadcasted_iota(jnp.int32, (SGU_BLOCK, SGU_BLOCK), 0) // CHUNK
    ci = lax.broadcasted_iota(jnp.int32, (SGU_BLOCK, SGU_BLOCK), 1) // CHUNK
    chunk_causal = ri >= ci
    n_groups = ws_ref.shape[0]
    gdim = da // n_groups
    per_tile = LANES // gdim
    wm = [jnp.where(chunk_causal, ws_ref[g], 0.0).astype(BF16) for g in range(n_groups)]
    lane = lax.broadcasted_iota(jnp.int32, (SGU_BLOCK, LANES), 1)
    for blk in range(t // SGU_BLOCK):
        rws = slice(blk * SGU_BLOCK, (blk + 1) * SGU_BLOCK)
        tiles = []
        for lt in range(da // LANES):
            vt = vn[rws, lt * LANES:(lt + 1) * LANES]
            mixed = jnp.dot(wm[lt * per_tile], vt, preferred_element_type=F32)
            for j in range(1, per_tile):
                mj = jnp.dot(wm[lt * per_tile + j], vt, preferred_element_type=F32)
                mixed = jnp.where(lane >= j * gdim, mj, mixed)
            tiles.append(mixed)
        mixed = jnp.concatenate(tiles, axis=1) + bs_ref[...]
        y_scr[rws, da:] = (ug[rws] * mixed).astype(BF16)

    o_ref[0] = x + jnp.dot(y_scr[...], wout_ref[...], preferred_element_type=F32)


def _mixer(x, g, win, cw, cb, wri, bri, lam, lng, lnb, ws, bs, wout):
    bsz, seq, d = x.shape
    da = cw.shape[1]
    tile = min(MIX_TILE, seq)
    assert seq % tile == 0 and tile % SGU_BLOCK == 0
    consts = (g, win, cw, cb, wri, bri, lam, lng, lnb, ws, bs, wout)
    return pl.pallas_call(
        _mixer_kernel,
        out_shape=jax.ShapeDtypeStruct(x.shape, x.dtype),
        grid=(bsz, seq // tile),
        in_specs=[pl.BlockSpec((1, tile, d), lambda b, s: (b, s, 0))] + [_resident(c.shape) for c in consts],
        out_specs=pl.BlockSpec((1, tile, d), lambda b, s: (b, s, 0)),
        scratch_shapes=[
            pltpu.VMEM((SUBLANES, da), F32),
            pltpu.VMEM((SUBLANES, da), F32),
            pltpu.VMEM((tile, 2 * da), BF16),
        ],
        compiler_params=pltpu.CompilerParams(
            dimension_semantics=("arbitrary", "arbitrary"), vmem_limit_bytes=VMEM_LIMIT),
        name="rglru_sgu_mixer",
    )(x, *consts)


def _attn_kernel(x_ref, g_ref, wqkv_ref, bias_ref, wo_ref, o_ref, k_scr, v_scr, a_scr):
    s = pl.program_id(1)
    x = x_ref[0]
    tq, d = x.shape
    n_heads = bias_ref.shape[0]
    hd = d // n_heads
    band = k_scr.shape[0]
    past = band - tq

    @pl.when(s == 0)
    def _():
        k_scr[...] = jnp.zeros_like(k_scr)
        v_scr[...] = jnp.zeros_like(v_scr)

    h = _rms(x, g_ref[...]).astype(BF16)
    per_tile = MXU_DIM // hd
    kpos = lax.broadcasted_iota(jnp.int32, (per_tile * tq, band), 1)
    valid = kpos >= past - s * tq
    lane = lax.broadcasted_iota(jnp.int32, (tq, MXU_DIM), 1)
    mine = [(lane >= j * hd) & (lane < (j + 1) * hd) for j in range(per_tile)]

    def scores(ht):
        cols = slice(ht * MXU_DIM, (ht + 1) * MXU_DIM)
        kcols = slice(d + ht * MXU_DIM, d + (ht + 1) * MXU_DIM)
        vcols = slice(2 * d + ht * MXU_DIM, 2 * d + (ht + 1) * MXU_DIM)
        qt = (jnp.dot(h, wqkv_ref[:, cols], preferred_element_type=F32) * hd ** -0.5).astype(BF16)
        kt = jnp.dot(h, wqkv_ref[:, kcols], preferred_element_type=F32).astype(BF16)
        vt = jnp.dot(h, wqkv_ref[:, vcols], preferred_element_type=F32).astype(BF16)
        k_scr[:past, cols] = k_scr[tq:, cols]
        v_scr[:past, cols] = v_scr[tq:, cols]
        k_scr[past:, cols] = kt
        v_scr[past:, cols] = vt
        qs = jnp.concatenate([jnp.where(m, qt, jnp.zeros_like(qt)) for m in mine], axis=0)
        return lax.dot_general(qs, k_scr[:, cols], (((1,), (1,)), ((), ())), preferred_element_type=F32)

    n_tiles = d // MXU_DIM
    sc_next = scores(0)
    for ht in range(n_tiles):
        cols = slice(ht * MXU_DIM, (ht + 1) * MXU_DIM)
        sc = sc_next
        if ht + 1 < n_tiles:
            sc_next = scores(ht + 1)
        bias = bias_ref[ht * per_tile:(ht + 1) * per_tile].reshape(per_tile * tq, band)
        sc = jnp.where(valid, sc + bias, NEG_INF)
        p = jnp.exp(sc - jnp.max(sc, axis=-1, keepdims=True))
        l = jnp.sum(p, axis=-1, keepdims=True)
        o = jnp.dot(p.astype(BF16), v_scr[:, cols], preferred_element_type=F32) * (1.0 / l)
        out = o[:tq]
        for j in range(1, per_tile):
            out = jnp.where(mine[j], o[j * tq:(j + 1) * tq], out)
        a_scr[:, cols] = out.astype(BF16)

    o_ref[0] = x + jnp.dot(a_scr[...], wo_ref[...], preferred_element_type=F32)


def _attn(x, g, wqkv, bias, wo):
    bsz, seq, d = x.shape
    tq = bias.shape[1]
    band = bias.shape[2]
    assert seq % tq == 0
    consts = (g, wqkv, bias, wo)
    return pl.pallas_call(
        _attn_kernel,
        out_shape=jax.ShapeDtypeStruct(x.shape, x.dtype),
        grid=(bsz, seq // tq),
        in_specs=[pl.BlockSpec((1, tq, d), lambda b, s: (b, s, 0))] + [_resident(c.shape) for c in consts],
        out_specs=pl.BlockSpec((1, tq, d), lambda b, s: (b, s, 0)),
        scratch_shapes=[
            pltpu.VMEM((band, d), BF16),
            pltpu.VMEM((band, d), BF16),
            pltpu.VMEM((tq, d), BF16),
        ],
        compiler_params=pltpu.CompilerParams(
            dimension_semantics=("arbitrary", "arbitrary"), vmem_limit_bytes=VMEM_LIMIT),
        name="band_attention",
    )(x, *consts)


def _band_bias(rel_bias, tq):
    past = N_PREV_CHUNKS * CHUNK
    band = past + tq
    nh = rel_bias.shape[0]
    n = band + tq - 1
    dist = band - 1 - jnp.arange(n)
    vec = rel_bias[:, jnp.clip(dist, -MAX_REL, MAX_REL) + MAX_REL].astype(F32)
    vec = jnp.roll(vec, -(tq - 1), axis=1)
    table = jnp.tile(vec, (1, tq))[:, :tq * (n - 1)].reshape(nh, tq, n - 1)[:, :, :band]
    dc = jnp.arange(band)[None, :] // CHUNK - jnp.arange(tq)[:, None] // CHUNK
    in_band = (dc >= 0) & (dc <= N_PREV_CHUNKS)
    return jnp.where(in_band[None], table, NEG_INF)


def _row(v):
    return v.reshape(1, -1)


def _pad_cols(w, n):
    return jnp.pad(w, ((0, 0), (0, n - w.shape[1])))


def _ffn_params(norm, w_up, conv_w, conv_b, w_down):
    dff = w_down.shape[0]
    fp = -(-dff // FFN_CHUNK) * FFN_CHUNK
    wg = _pad_cols(w_up[:, :dff].astype(BF16), fp)
    wu = _pad_cols(w_up[:, dff:].astype(BF16), fp)
    cwg = _pad_cols(conv_w[:, :dff], fp)
    cwu = _pad_cols(conv_w[:, dff:], fp)
    cbg = _pad_cols(_row(conv_b[:dff]), fp)
    cbu = _pad_cols(_row(conv_b[dff:]), fp)
    wd = jnp.pad(w_down.astype(BF16), ((0, fp - dff), (0, 0)))
    return _row(norm), wg, wu, cwg, cwu, cbg, cbu, wd


def _block_diag(w):
    nh, di, dj = w.shape
    eye = jnp.eye(nh, dtype=w.dtype)
    return (eye[:, None, :, None] * w[:, :, None, :]).reshape(nh * di, nh * dj)


def kernel(x, norm_ab, w_in_ab, lru_conv_w, lru_conv_b, lru_gate_r_w, lru_gate_r_b, lru_gate_i_w, lru_gate_i_b, lru_lambda, sgu_ln_g, sgu_ln_b, sgu_w, sgu_b, w_out_ab, norm_attn, w_qkv, rel_bias, w_o, norm_ffn, ffn_w_up, ffn_conv_w, ffn_conv_b, ffn_w_down, norm_final):
    depth = norm_ffn.shape[0]
    for layer in range(depth):
        if layer % 2 == 0:
            e = layer // 2
            wri = jnp.concatenate([_block_diag(lru_gate_r_w[e]), _block_diag(lru_gate_i_w[e])], axis=1)
            bri = _row(jnp.concatenate([lru_gate_r_b[e], lru_gate_i_b[e]]))
            gdim = sgu_ln_g.shape[1] // sgu_w.shape[1]
            bs = jnp.repeat(sgu_b[e].T, gdim, axis=1)
            x = _mixer(x, _row(norm_ab[e]), w_in_ab[e].astype(BF16), lru_conv_w[e], _row(lru_conv_b[e]),
                       wri.astype(BF16), bri, _row(lru_lambda[e]), _row(sgu_ln_g[e]), _row(sgu_ln_b[e]),
                       sgu_w[e], bs, w_out_ab[e].astype(BF16))
        else:
            o = layer // 2
            x = _attn(x, _row(norm_attn[o]), w_qkv[o].astype(BF16), _band_bias(rel_bias[o], ATTN_TILE),
                      w_o[o].astype(BF16))
        last = layer == depth - 1
        x = _ffn(x, *_ffn_params(norm_ffn[layer], ffn_w_up[layer], ffn_conv_w[layer], ffn_conv_b[layer],
                                 ffn_w_down[layer]), _row(norm_final), final_norm=last)
    return x
```
